```python
import math
import jax, jax.numpy as jnp
from jax import lax
import numpy as np

D_MODEL = 1024
BATCH = 8
SEQ = 4096
DEPTH = 4

CHUNK = 64
Q_BLOCK = 128
ROPE_THETA = 10000.0
NORM_EPS = 1e-6
SUBLN_EPS = 1e-5
N_A_LAYERS = DEPTH // 2
N_B_LAYERS = DEPTH - N_A_LAYERS

MLA_HEADS = D_MODEL // 128
MLA_NOPE = 128
MLA_ROPE = 64
MLA_V = 128
MLA_Q_LORA = 3 * D_MODEL // 8
MLA_KV_LORA = D_MODEL // 4

DIFF_HEADS = D_MODEL // 128
DIFF_HEAD_DIM = 64
DIFF_V_DIM = 2 * DIFF_HEAD_DIM

D_FF = 11 * D_MODEL // 4
CONV_WIDTH = 3

kernel_name = "yoco_mla_diffattn_convffn_trunk"


def rmsnorm(x, g, eps=NORM_EPS):
    xf = x.astype(jnp.float32)
    y = xf * lax.rsqrt(jnp.mean(xf * xf, axis=-1, keepdims=True) + eps)
    return (y * g.astype(jnp.float32)).astype(x.dtype)


def rope_table(dim, seq):
    pos = jnp.arange(seq, dtype=jnp.float32)
    inv = ROPE_THETA ** (-jnp.arange(0, dim, 2, dtype=jnp.float32) / dim)
    ang = pos[:, None] * inv[None, :]
    ang = jnp.concatenate([ang, ang], axis=-1)
    return jnp.cos(ang), jnp.sin(ang)


def rope(x, cos, sin):
    xf = x.astype(jnp.float32)
    x1, x2 = jnp.split(xf, 2, axis=-1)
    rot = jnp.concatenate([-x2, x1], axis=-1)
    return (xf * cos + rot * sin).astype(x.dtype)


def chunk_causal_mask(q_start, seq):
    q_chunk = (q_start + jnp.arange(Q_BLOCK, dtype=jnp.int32)) // CHUNK
    k_chunk = jnp.arange(seq, dtype=jnp.int32) // CHUNK
    return k_chunk[None, :] <= q_chunk[:, None]


def block_sweep(fn, *q_arrays):
    b, s = q_arrays[0].shape[:2]
    nb = s // Q_BLOCK
    blocks = tuple(jnp.moveaxis(a.reshape((b, nb, Q_BLOCK) + a.shape[2:]), 1, 0) for a in q_arrays)
    starts = jnp.arange(nb, dtype=jnp.int32) * Q_BLOCK
    out = lax.map(lambda args: fn(args[0], *args[1:]), (starts,) + blocks)
    out = jnp.moveaxis(out, 0, 1)
    return out.reshape((b, s) + out.shape[3:])


def mla_attention(h, w_down, g_q_lat, w_uq, g_kv_lat, w_ukv, w_o, cos, sin):
    b, s, _ = h.shape
    lat = h @ w_down
    c_q = rmsnorm(lat[..., :MLA_Q_LORA], g_q_lat)
    c_kv = rmsnorm(lat[..., MLA_Q_LORA:MLA_Q_LORA + MLA_KV_LORA], g_kv_lat)
    k_pe = rope(lat[..., MLA_Q_LORA + MLA_KV_LORA:], cos, sin)
    q = (c_q @ w_uq).reshape(b, s, MLA_HEADS, MLA_NOPE + MLA_ROPE)
    q_nope = q[..., :MLA_NOPE]
    q_pe = rope(q[..., MLA_NOPE:], cos[:, None, :], sin[:, None, :])
    kv = (c_kv @ w_ukv).reshape(b, s, MLA_HEADS, MLA_NOPE + MLA_V)
    k_nope32 = kv[..., :MLA_NOPE].astype(jnp.float32)
    v32 = kv[..., MLA_NOPE:].astype(jnp.float32)
    k_pe32 = k_pe.astype(jnp.float32)
    scale = (MLA_NOPE + MLA_ROPE) ** -0.5
    neg = jnp.finfo(jnp.float32).min

    def block(q_start, qn, qp):
        sc = (jnp.einsum('bqhd,bkhd->bhqk', qn.astype(jnp.float32), k_nope32)
              + jnp.einsum('bqhr,bkr->bhqk', qp.astype(jnp.float32), k_pe32)) * scale
        sc = jnp.where(chunk_causal_mask(q_start, s), sc, neg)
        p = jax.nn.softmax(sc, axis=-1)
        return jnp.einsum('bhqk,bkhd->bqhd', p, v32).astype(h.dtype)

    o = block_sweep(block, q_nope, q_pe)
    return o.reshape(b, s, MLA_HEADS * MLA_V) @ w_o


def shared_kv(x, g_kv, w_kv, cos, sin):
    b, s, _ = x.shape
    kv = rmsnorm(x, g_kv) @ w_kv
    nk = DIFF_HEADS * 2 * DIFF_HEAD_DIM
    k = kv[..., :nk].reshape(b, s, DIFF_HEADS, 2, DIFF_HEAD_DIM)
    k = rope(k, cos[:, None, None, :], sin[:, None, None, :])
    v = kv[..., nk:].reshape(b, s, DIFF_HEADS, DIFF_V_DIM)
    return k, v


def diff_attention(h, k, v, w_q, lam_params, g_subln, w_o, lambda_init, cos, sin):
    b, s, _ = h.shape
    q = (h @ w_q).reshape(b, s, DIFF_HEADS, 2, DIFF_HEAD_DIM)
    q = rope(q, cos[:, None, None, :], sin[:, None, None, :])
    lp = lam_params.astype(jnp.float32)
    lam = jnp.exp(jnp.sum(lp[0] * lp[1])) - jnp.exp(jnp.sum(lp[2] * lp[3])) + lambda_init
    k32 = k.astype(jnp.float32)
    v32 = v.astype(jnp.float32)
    scale = DIFF_HEAD_DIM ** -0.5
    neg = jnp.finfo(jnp.float32).min

    def block(q_start, qb):
        sc = jnp.einsum('bqhcd,bkhcd->bchqk', qb.astype(jnp.float32), k32) * scale
        sc = jnp.where(chunk_causal_mask(q_start, s), sc, neg)
        p = jax.nn.softmax(sc, axis=-1)
        a = p[:, 0] - lam * p[:, 1]
        return jnp.einsum('bhqk,bkhd->bqhd', a, v32)

    o = block_sweep(block, q)
    o = rmsnorm(o, g_subln, SUBLN_EPS) * (1.0 - lambda_init)
    return o.astype(h.dtype).reshape(b, s, DIFF_HEADS * DIFF_V_DIM) @ w_o


def conv_ffn(h, w_up, conv_w, conv_b, w_down):
    u = h @ w_up
    u = lax.conv_general_dilated(
        u, conv_w[:, None, :].astype(u.dtype), window_strides=(1,),
        padding=[(CONV_WIDTH - 1, 0)], dimension_numbers=('NWC', 'WIO', 'NWC'),
        feature_group_count=u.shape[-1]) + conv_b
    gate, val = jnp.split(u, 2, axis=-1)
    return (jax.nn.silu(gate) * val) @ w_down


def diff_lambda_init(layer):
    return 0.8 - 0.6 * math.exp(-0.3 * layer)


def setup_inputs(seed: int = 0) -> dict:
    key = jax.random.key(seed)
    ks = iter(jax.random.split(key, 32))

    def w(shape, fan_in):
        return jax.random.normal(next(ks), shape, jnp.float32) * fan_in ** -0.5

    def gain(shape):
        return 1.0 + 0.02 * jax.random.normal(next(ks), shape, jnp.float32)

    d_lat = MLA_Q_LORA + MLA_KV_LORA + MLA_ROPE
    x = jax.random.normal(next(ks), (BATCH, SEQ, D_MODEL), jnp.float32)
    return {
        "x": x,
        "g_attn_pre": gain((DEPTH, D_MODEL)),
        "g_attn_post": gain((DEPTH, D_MODEL)),
        "g_ffn_pre": gain((DEPTH, D_MODEL)),
        "g_ffn_post": gain((DEPTH, D_MODEL)),
        "mla_w_down": w((N_A_LAYERS, D_MODEL, d_lat), D_MODEL),
        "mla_g_q_lat": gain((N_A_LAYERS, MLA_Q_LORA)),
        "mla_w_uq": w((N_A_LAYERS, MLA_Q_LORA, MLA_HEADS * (MLA_NOPE + MLA_ROPE)), MLA_Q_LORA),
        "mla_g_kv_lat": gain((N_A_LAYERS, MLA_KV_LORA)),
        "mla_w_ukv": w((N_A_LAYERS, MLA_KV_LORA, MLA_HEADS * (MLA_NOPE + MLA_V)), MLA_KV_LORA),
        "mla_w_o": w((N_A_LAYERS, MLA_HEADS * MLA_V, D_MODEL), MLA_HEADS * MLA_V),
        "shared_g_kv": gain((D_MODEL,)),
        "shared_w_kv": w((D_MODEL, DIFF_HEADS * (2 * DIFF_HEAD_DIM + DIFF_V_DIM)), D_MODEL),
        "diff_w_q": w((N_B_LAYERS, D_MODEL, DIFF_HEADS * 2 * DIFF_HEAD_DIM), D_MODEL),
        "diff_lambda": 0.1 * jax.random.normal(next(ks), (N_B_LAYERS, 4, DIFF_HEAD_DIM), jnp.float32),
        "diff_g_subln": gain((N_B_LAYERS, DIFF_V_DIM)),
        "diff_w_o": w((N_B_LAYERS, DIFF_HEADS * DIFF_V_DIM, D_MODEL), DIFF_HEADS * DIFF_V_DIM),
        "ffn_w_up": w((DEPTH, D_MODEL, 2 * D_FF), D_MODEL),
        "ffn_conv_w": w((DEPTH, CONV_WIDTH, 2 * D_FF), CONV_WIDTH),
        "ffn_conv_b": 0.01 * jax.random.normal(next(ks), (DEPTH, 2 * D_FF), jnp.float32),
        "ffn_w_down": w((DEPTH, D_FF, D_MODEL), D_FF),
    }


def reference(x, g_attn_pre, g_attn_post, g_ffn_pre, g_ffn_post,
              mla_w_down, mla_g_q_lat, mla_w_uq, mla_g_kv_lat, mla_w_ukv, mla_w_o,
              shared_g_kv, shared_w_kv,
              diff_w_q, diff_lambda, diff_g_subln, diff_w_o,
              ffn_w_up, ffn_conv_w, ffn_conv_b, ffn_w_down):
    seq = x.shape[1]
    cos_a, sin_a = rope_table(MLA_ROPE, seq)
    cos_b, sin_b = rope_table(DIFF_HEAD_DIM, seq)
    k_shared, v_shared = None, None
    for layer in range(DEPTH):
        if layer == N_A_LAYERS:
            k_shared, v_shared = shared_kv(x, shared_g_kv, shared_w_kv, cos_b, sin_b)
        h = rmsnorm(x, g_attn_pre[layer])
        if layer < N_A_LAYERS:
            i = layer
            a = mla_attention(h, mla_w_down[i], mla_g_q_lat[i], mla_w_uq[i],
                              mla_g_kv_lat[i], mla_w_ukv[i], mla_w_o[i], cos_a, sin_a)
        else:
            j = layer - N_A_LAYERS
            a = diff_attention(h, k_shared, v_shared, diff_w_q[j], diff_lambda[j],
                               diff_g_subln[j], diff_w_o[j], diff_lambda_init(layer),
                               cos_b, sin_b)
        x = x + rmsnorm(a, g_attn_post[layer])
        h = rmsnorm(x, g_ffn_pre[layer])
        f = conv_ffn(h, ffn_w_up[layer], ffn_conv_w[layer], ffn_conv_b[layer], ffn_w_down[layer])
        x = x + rmsnorm(f, g_ffn_post[layer])
    return x
```

```python
import functools
import math

import jax
import jax.numpy as jnp
from jax import lax
from jax.experimental import pallas as pl
from jax.experimental.pallas import tpu as pltpu

F32 = jnp.float32
BF16 = jnp.bfloat16

D_MODEL = 1024
DEPTH = 4
CHUNK = 64
ROPE_THETA = 10000.0
NORM_EPS = 1e-6
SUBLN_EPS = 1e-5
N_A_LAYERS = DEPTH // 2
HEADS = D_MODEL // 128
MLA_NOPE = 128
MLA_ROPE = 64
MLA_V = 128
MLA_Q_LORA = 3 * D_MODEL // 8
MLA_KV_LORA = D_MODEL // 4
DIFF_HEAD_DIM = 64
DIFF_V_DIM = 2 * DIFF_HEAD_DIM
D_FF = 11 * D_MODEL // 4
CONV_WIDTH = 3

LANES = 128
BF16_ROWS = 16
ROW_TILE = 512
FFN_CHUNK = 256
VMEM_LIMIT = 56 * 1024 * 1024

NEG = float(jnp.finfo(jnp.float32).min)
NT_DIMS = (((1,), (1,)), ((), ()))
TN_DIMS = (((0,), (0,)), ((), ()))


def _rms(x, g, eps):
    return x * lax.rsqrt(jnp.mean(x * x, axis=-1, keepdims=True) + eps) * g


def _rot_cols(w):
    g = w.reshape(w.shape[:-1] + (-1, 2, 32))
    return jnp.concatenate([-g[..., 1, :], g[..., 0, :]], axis=-1).reshape(w.shape)


def _params(sem):
    return pltpu.CompilerParams(dimension_semantics=sem, vmem_limit_bytes=VMEM_LIMIT)


def _const_spec(shape):
    nd = len(shape)
    return pl.BlockSpec(shape, lambda *_: (0,) * nd)


def _mla_proj_kernel(x_ref, gpre_ref, wd_ref, gq_ref, gkv_ref, wuq_ref, wuk_ref, wuvt_ref,
                     cos_ref, sin_ref, q_ref, k_ref, vt_ref, *, scale):
    tm = x_ref.shape[1]
    h = _rms(x_ref[0], gpre_ref[...], NORM_EPS).astype(BF16)
    lat = jnp.dot(h, wd_ref[...], preferred_element_type=F32)
    c_q = _rms(lat[:, :MLA_Q_LORA], gq_ref[...], NORM_EPS).astype(BF16)
    kv0 = MLA_Q_LORA
    c_kv = _rms(lat[:, kv0:kv0 + MLA_KV_LORA], gkv_ref[...], NORM_EPS).astype(BF16)
    cos = cos_ref[...]
    sin = sin_ref[...]
    pe0 = kv0 + MLA_KV_LORA
    kpe2 = lat[:, pe0:pe0 + LANES] * cos + lat[:, pe0 + LANES:pe0 + 2 * LANES] * sin
    lane = lax.broadcasted_iota(jnp.int32, (tm, LANES), 1)
    kpe_slot = (jnp.where(lane < MLA_ROPE, kpe2, 0.0).astype(BF16),
                jnp.where(lane >= MLA_ROPE, kpe2, 0.0).astype(BF16))

    qall = jnp.dot(c_q, wuq_ref[...], preferred_element_type=F32)
    kn = jnp.dot(c_kv, wuk_ref[...], preferred_element_type=F32)
    vt = lax.dot_general(wuvt_ref[...], c_kv, NT_DIMS, preferred_element_type=F32)

    n_nope = HEADS * MLA_NOPE
    n_pe = HEADS * MLA_ROPE
    for j in range(HEADS // 2):
        a = qall[:, n_nope + j * LANES:n_nope + (j + 1) * LANES]
        b = qall[:, n_nope + n_pe + j * LANES:n_nope + n_pe + (j + 1) * LANES]
        qpe = ((a * cos + b * sin) * scale).astype(BF16)
        for hh in (2 * j, 2 * j + 1):
            q_ref[0, hh, 0, :, 0:LANES] = (qall[:, hh * LANES:(hh + 1) * LANES] * scale).astype(BF16)
            q_ref[0, hh, 0, :, LANES:2 * LANES] = qpe
            k_ref[0, hh, 0, :, 0:LANES] = kn[:, hh * LANES:(hh + 1) * LANES].astype(BF16)
            k_ref[0, hh, 0, :, LANES:2 * LANES] = kpe_slot[hh % 2]
    vt_ref[0, :, 0] = vt.reshape(HEADS, MLA_V, tm).astype(BF16)


def _mla_proj(x, g_pre, wd, g_q, g_kv, wuq, wuk, wuvt, cos2, sin2):
    b, s, d = x.shape
    tm = ROW_TILE
    nt = s // tm
    scale = (MLA_NOPE + MLA_ROPE) ** -0.5
    out_shape = (
        jax.ShapeDtypeStruct((b, HEADS, nt, tm, 2 * LANES), BF16),
        jax.ShapeDtypeStruct((b, HEADS, nt, tm, 2 * LANES), BF16),
        jax.ShapeDtypeStruct((b, HEADS, nt, MLA_V, tm), BF16),
    )
    return pl.pallas_call(
        functools.partial(_mla_proj_kernel, scale=scale),
        grid=(b, nt),
        in_specs=[
            pl.BlockSpec((1, tm, d), lambda i, j: (i, j, 0)),
            _const_spec(g_pre.shape), _const_spec(wd.shape), _const_spec(g_q.shape), _const_spec(g_kv.shape),
            _const_spec(wuq.shape), _const_spec(wuk.shape), _const_spec(wuvt.shape),
            pl.BlockSpec((tm, LANES), lambda i, j: (j, 0)),
            pl.BlockSpec((tm, LANES), lambda i, j: (j, 0)),
        ],
        out_specs=(
            pl.BlockSpec((1, HEADS, 1, tm, 2 * LANES), lambda i, j: (i, 0, j, 0, 0)),
            pl.BlockSpec((1, HEADS, 1, tm, 2 * LANES), lambda i, j: (i, 0, j, 0, 0)),
            pl.BlockSpec((1, HEADS, 1, MLA_V, tm), lambda i, j: (i, 0, j, 0, 0)),
        ),
        out_shape=out_shape,
        compiler_params=_params(("parallel", "parallel")),
        name="mla_proj",
    )(x, g_pre, wd, g_q, g_kv, wuq, wuk, wuvt, cos2, sin2)


def _rope_proj_kernel(*refs, scale, with_v):
    if with_v:
        x_ref, g_ref, w_ref, wvt_ref, cos_ref, sin_ref, o_ref, vt_ref = refs
    else:
        x_ref, g_ref, w_ref, cos_ref, sin_ref, o_ref = refs
    tm = x_ref.shape[1]
    h = _rms(x_ref[0], g_ref[...], NORM_EPS).astype(BF16)
    y = jnp.dot(h, w_ref[...], preferred_element_type=F32)
    cos = cos_ref[...]
    sin = sin_ref[...]
    n = HEADS * LANES
    for hh in range(HEADS):
        a = y[:, hh * LANES:(hh + 1) * LANES]
        b = y[:, n + hh * LANES:n + (hh + 1) * LANES]
        o_ref[0, hh, 0] = ((a * cos + b * sin) * scale).astype(BF16)
    if with_v:
        vt = lax.dot_general(wvt_ref[...], h, NT_DIMS, preferred_element_type=F32)
        vt_ref[0, :, 0] = vt.reshape(HEADS, DIFF_V_DIM, tm).astype(BF16)


def _rope_proj(x, g, w, wvt, cos2, sin2, scale):
    b, s, d = x.shape
    tm = ROW_TILE
    nt = s // tm
    with_v = wvt is not None
    in_specs = [pl.BlockSpec((1, tm, d), lambda i, j: (i, j, 0)), _const_spec(g.shape), _const_spec(w.shape)]
    args = [x, g, w]
    if with_v:
        in_specs.append(_const_spec(wvt.shape))
        args.append(wvt)
    in_specs += [pl.BlockSpec((tm, LANES), lambda i, j: (j, 0)), pl.BlockSpec((tm, LANES), lambda i, j: (j, 0))]
    args += [cos2, sin2]
    out_shape = [jax.ShapeDtypeStruct((b, HEADS, nt, tm, LANES), BF16)]
    out_specs = [pl.BlockSpec((1, HEADS, 1, tm, LANES), lambda i, j: (i, 0, j, 0, 0))]
    if with_v:
        out_shape.append(jax.ShapeDtypeStruct((b, HEADS, nt, DIFF_V_DIM, tm), BF16))
        out_specs.append(pl.BlockSpec((1, HEADS, 1, DIFF_V_DIM, tm), lambda i, j: (i, 0, j, 0, 0)))
    out = pl.pallas_call(
        functools.partial(_rope_proj_kernel, scale=scale, with_v=with_v),
        grid=(b, nt),
        in_specs=in_specs,
        out_specs=tuple(out_specs),
        out_shape=tuple(out_shape),
        compiler_params=_params(("parallel", "parallel")),
        name="shared_kv_proj" if with_v else "diff_q_proj",
    )(*args)
    return out if with_v else out[0]


def _chunk_mask(t):
    shift = CHUNK.bit_length() - 1
    kk = lax.broadcasted_iota(jnp.int32, (t, t), 0) >> shift
    qq = lax.broadcasted_iota(jnp.int32, (t, t), 1) >> shift
    return kk <= qq


def _softmax_step(s, vt, m, l, acc):
    m_new = jnp.maximum(m, jnp.max(s, axis=0, keepdims=True))
    p = jnp.exp(s - m_new)
    alpha = jnp.exp(m - m_new)
    l = alpha * l + jnp.sum(p, axis=0, keepdims=True)
    acc = alpha * acc + jnp.dot(vt, p.astype(BF16), preferred_element_type=F32)
    return m_new, l, acc


def _mla_attn_kernel(q_ref, k_ref, vt_ref, o_ref):
    i = pl.program_id(2)
    t = q_ref.shape[3]
    q = q_ref[0, 0, 0]

    def tile(j, carry, masked):
        s = lax.dot_general(k_ref[0, 0, j], q, NT_DIMS, preferred_element_type=F32)
        if masked:
            s = jnp.where(_chunk_mask(t), s, NEG)
        return _softmax_step(s, vt_ref[0, 0, j], *carry)

    init = (jnp.full((1, t), NEG, F32), jnp.zeros((1, t), F32), jnp.zeros((MLA_V, t), F32))
    carry = lax.fori_loop(0, i, lambda j, c: tile(j, c, False), init)
    _, l, acc = tile(i, carry, True)
    o_ref[0, 0] = (acc / l).astype(BF16)


def _mla_attn(q, k, vt):
    b, hn, nt, t, dq = q.shape
    s = nt * t
    return pl.pallas_call(
        _mla_attn_kernel,
        grid=(b, hn, nt),
        in_specs=[
            pl.BlockSpec((1, 1, 1, t, dq), lambda i, h, j: (i, h, j, 0, 0)),
            pl.BlockSpec((1, 1, nt, t, dq), lambda i, h, j: (i, h, 0, 0, 0)),
            pl.BlockSpec((1, 1, nt, MLA_V, t), lambda i, h, j: (i, h, 0, 0, 0)),
        ],
        out_specs=pl.BlockSpec((1, 1, MLA_V, t), lambda i, h, j: (i, h, 0, j)),
        out_shape=jax.ShapeDtypeStruct((b, hn, MLA_V, s), BF16),
        compiler_params=_params(("parallel", "parallel", "arbitrary")),
        name="mla_attn",
    )(q, k, vt)


def _diff_attn_kernel(lam_ref, q_ref, k_ref, vt_ref, g_ref, o_ref, *, lambda_init):
    i = pl.program_id(2)
    t = q_ref.shape[3]
    q = q_ref[0, 0, 0]
    lane = lax.broadcasted_iota(jnp.int32, q.shape, 1)
    zero = jnp.zeros_like(q)
    q0 = jnp.where(lane < DIFF_HEAD_DIM, q, zero)
    q1 = jnp.where(lane >= DIFF_HEAD_DIM, q, zero)

    def tile(j, carry, masked):
        kt = k_ref[0, 0, j]
        vt = vt_ref[0, 0, j]
        s0 = lax.dot_general(kt, q0, NT_DIMS, preferred_element_type=F32)
        s1 = lax.dot_general(kt, q1, NT_DIMS, preferred_element_type=F32)
        if masked:
            mask = _chunk_mask(t)
            s0 = jnp.where(mask, s0, NEG)
            s1 = jnp.where(mask, s1, NEG)
        return _softmax_step(s0, vt, *carry[:3]) + _softmax_step(s1, vt, *carry[3:])

    one = (jnp.full((1, t), NEG, F32), jnp.zeros((1, t), F32), jnp.zeros((DIFF_V_DIM, t), F32))
    carry = lax.fori_loop(0, i, lambda j, c: tile(j, c, False), one + one)
    _, l0, acc0, _, l1, acc1 = tile(i, carry, True)

    lp = lam_ref[...]
    lam = (jnp.exp(jnp.sum(lp[0:1] * lp[1:2], axis=1, keepdims=True))
           - jnp.exp(jnp.sum(lp[2:3] * lp[3:4], axis=1, keepdims=True)) + lambda_init)
    o = acc0 / l0 - lam * (acc1 / l1)
    o = o * lax.rsqrt(jnp.mean(o * o, axis=0, keepdims=True) + SUBLN_EPS) * g_ref[...]
    o_ref[0, 0] = (o * (1.0 - lambda_init)).astype(BF16)


def _diff_attn(lam_params, q, k, vt, g_col, lambda_init):
    b, hn, nt, t, dq = q.shape
    s = nt * t
    return pl.pallas_call(
        functools.partial(_diff_attn_kernel, lambda_init=lambda_init),
        grid=(b, hn, nt),
        in_specs=[
            _const_spec(lam_params.shape),
            pl.BlockSpec((1, 1, 1, t, dq), lambda i, h, j: (i, h, j, 0, 0)),
            pl.BlockSpec((1, 1, nt, t, dq), lambda i, h, j: (i, h, 0, 0, 0)),
            pl.BlockSpec((1, 1, nt, DIFF_V_DIM, t), lambda i, h, j: (i, h, 0, 0, 0)),
            _const_spec(g_col.shape),
        ],
        out_specs=pl.BlockSpec((1, 1, DIFF_V_DIM, t), lambda i, h, j: (i, h, 0, j)),
        out_shape=jax.ShapeDtypeStruct((b, hn, DIFF_V_DIM, s), BF16),
        compiler_params=_params(("parallel", "parallel", "arbitrary")),
        name="diff_attn",
    )(lam_params, q, k, vt, g_col)


def _post_attn_kernel(ot_ref, wo_ref, x_ref, gpost_ref, gffn_ref, xo_ref, h_ref):
    a = lax.dot_general(ot_ref[0], wo_ref[...], TN_DIMS, preferred_element_type=F32)
    x = x_ref[0] + _rms(a, gpost_ref[...], NORM_EPS)
    xo_ref[0] = x
    h_ref[0] = _rms(x, gffn_ref[...], NORM_EPS).astype(BF16)


def _post_attn(ot, wo, x, g_post, g_ffn):
    b, s, d = x.shape
    tm = ROW_TILE
    return pl.pallas_call(
        _post_attn_kernel,
        grid=(b, s // tm),
        in_specs=[
            pl.BlockSpec((1, d, tm), lambda i, j: (i, 0, j)),
            _const_spec(wo.shape),
            pl.BlockSpec((1, tm, d), lambda i, j: (i, j, 0)),
            _const_spec(g_post.shape), _const_spec(g_ffn.shape),
        ],
        out_specs=(pl.BlockSpec((1, tm, d), lambda i, j: (i, j, 0)),
                   pl.BlockSpec((1, tm, d), lambda i, j: (i, j, 0))),
        out_shape=(jax.ShapeDtypeStruct((b, s, d), F32), jax.ShapeDtypeStruct((b, s, d), BF16)),
        compiler_params=_params(("parallel", "parallel")),
        name="post_attn",
    )(ot, wo, x, g_post, g_ffn)


def _ffn_kernel(h_ref, halo_ref, x_ref, wup_ref, cw_ref, cb_ref, wdn_ref, g_ref, o_ref, hext_ref, acc_ref):
    tm = h_ref.shape[1]
    nchunk = wdn_ref.shape[0]
    halo = halo_ref[0]
    hext_ref[0:BF16_ROWS, :] = jnp.where(pl.program_id(1) > 0, halo, jnp.zeros_like(halo))
    hext_ref[BF16_ROWS:, :] = h_ref[0]
    acc_ref[...] = jnp.zeros_like(acc_ref)

    def conv(u, w, bias):
        y = w[0:1] * pltpu.roll(u, 2, 0) + w[1:2] * pltpu.roll(u, 1, 0) + w[2:3] * u
        return y[BF16_ROWS:] + bias

    def chunk(c, _):
        hx = hext_ref[...]
        ug = jnp.dot(hx, wup_ref[c], preferred_element_type=F32)
        uv = jnp.dot(hx, wup_ref[nchunk + c], preferred_element_type=F32)
        gate = conv(ug, cw_ref[c], cb_ref[c])
        val = conv(uv, cw_ref[nchunk + c], cb_ref[nchunk + c])
        g = (jax.nn.silu(gate) * val).astype(BF16)
        acc_ref[...] += jnp.dot(g, wdn_ref[c], preferred_element_type=F32)
        return 0

    lax.fori_loop(0, nchunk, chunk, 0)
    o_ref[0] = x_ref[0] + _rms(acc_ref[...], g_ref[...], NORM_EPS)


def _ffn(h, x, wup, cw, cb, wdn, g_post):
    b, s, d = x.shape
    tm = ROW_TILE
    per = tm // BF16_ROWS
    resident = dict(pipeline_mode=pl.Buffered(1))
    return pl.pallas_call(
        _ffn_kernel,
        grid=(b, s // tm),
        in_specs=[
            pl.BlockSpec((1, tm, d), lambda i, j: (i, j, 0)),
            pl.BlockSpec((1, BF16_ROWS, d), lambda i, j: (i, jnp.maximum(j * per - 1, 0), 0)),
            pl.BlockSpec((1, tm, d), lambda i, j: (i, j, 0)),
            pl.BlockSpec(wup.shape, lambda i, j: (0, 0, 0), **resident),
            _const_spec(cw.shape), _const_spec(cb.shape),
            pl.BlockSpec(wdn.shape, lambda i, j: (0, 0, 0), **resident),
            _const_spec(g_post.shape),
        ],
        out_specs=pl.BlockSpec((1, tm, d), lambda i, j: (i, j, 0)),
        out_shape=jax.ShapeDtypeStruct((b, s, d), F32),
        scratch_shapes=[pltpu.VMEM((tm + BF16_ROWS, d), BF16), pltpu.VMEM((tm, d), F32)],
        compiler_params=_params(("parallel", "arbitrary")),
        name="conv_ffn",
    )(h, h, x, wup, cw, cb, wdn, g_post)


def _row(v):
    return v.reshape(1, -1).astype(F32)


def _rope_tables(seq):
    pos = jnp.arange(seq, dtype=F32)
    inv = ROPE_THETA ** (-jnp.arange(0, MLA_ROPE, 2, dtype=F32) / MLA_ROPE)
    ang = pos[:, None] * inv[None, :]
    ang = jnp.concatenate([ang, ang, ang, ang], axis=-1)
    return jnp.cos(ang), jnp.sin(ang)


def _diff_lambda_init(layer):
    return 0.8 - 0.6 * math.exp(-0.3 * layer)


def _prep_mla(w_down, w_uq, w_ukv):
    pe = w_down[:, MLA_Q_LORA + MLA_KV_LORA:]
    pe_rot = _rot_cols(pe)
    wd = jnp.concatenate([w_down[:, :MLA_Q_LORA + MLA_KV_LORA], pe, pe, pe_rot, pe_rot], axis=1).astype(BF16)
    uq = w_uq.reshape(MLA_Q_LORA, HEADS, MLA_NOPE + MLA_ROPE)
    uq_pe = uq[:, :, MLA_NOPE:].reshape(MLA_Q_LORA, HEADS * MLA_ROPE)
    wuq = jnp.concatenate([uq[:, :, :MLA_NOPE].reshape(MLA_Q_LORA, HEADS * MLA_NOPE), uq_pe, _rot_cols(uq_pe)],
                          axis=1).astype(BF16)
    ukv = w_ukv.reshape(MLA_KV_LORA, HEADS, MLA_NOPE + MLA_V)
    wuk = ukv[:, :, :MLA_NOPE].reshape(MLA_KV_LORA, HEADS * MLA_NOPE).astype(BF16)
    wuvt = ukv[:, :, MLA_NOPE:].reshape(MLA_KV_LORA, HEADS * MLA_V).T.astype(BF16)
    return wd, wuq, wuk, wuvt


def _prep_ffn(w_up, conv_w, conv_b, w_down):
    n = 2 * D_FF // FFN_CHUNK
    wup = w_up.reshape(D_MODEL, n, FFN_CHUNK).transpose(1, 0, 2).astype(BF16)
    cw = conv_w.reshape(CONV_WIDTH, n, FFN_CHUNK).transpose(1, 0, 2).astype(F32)
    cb = conv_b.reshape(n, 1, FFN_CHUNK).astype(F32)
    wdn = w_down.reshape(D_FF // FFN_CHUNK, FFN_CHUNK, D_MODEL).astype(BF16)
    return wup, cw, cb, wdn


def kernel(x, g_attn_pre, g_attn_post, g_ffn_pre, g_ffn_post, mla_w_down, mla_g_q_lat, mla_w_uq, mla_g_kv_lat, mla_w_ukv, mla_w_o, shared_g_kv, shared_w_kv, diff_w_q, diff_lambda, diff_g_subln, diff_w_o, ffn_w_up, ffn_conv_w, ffn_conv_b, ffn_w_down):
    b, s, d = x.shape
    assert d == D_MODEL and s % ROW_TILE == 0
    cos2, sin2 = _rope_tables(s)
    diff_scale = DIFF_HEAD_DIM ** -0.5
    k_sh = vt_sh = None
    for layer in range(DEPTH):
        if layer < N_A_LAYERS:
            wd, wuq, wuk, wuvt = _prep_mla(mla_w_down[layer], mla_w_uq[layer], mla_w_ukv[layer])
            q, k, vt = _mla_proj(x, _row(g_attn_pre[layer]), wd, _row(mla_g_q_lat[layer]),
                                 _row(mla_g_kv_lat[layer]), wuq, wuk, wuvt, cos2, sin2)
            ot = _mla_attn(q, k, vt)
            wo = mla_w_o[layer].astype(BF16)
        else:
            jdx = layer - N_A_LAYERS
            if layer == N_A_LAYERS:
                nk = HEADS * 2 * DIFF_HEAD_DIM
                wk = shared_w_kv[:, :nk]
                wkk = jnp.concatenate([wk, _rot_cols(wk)], axis=1).astype(BF16)
                wvt = shared_w_kv[:, nk:].T.astype(BF16)
                k_sh, vt_sh = _rope_proj(x, _row(shared_g_kv), wkk, wvt, cos2, sin2, 1.0)
            wq = diff_w_q[jdx]
            wqq = jnp.concatenate([wq, _rot_cols(wq)], axis=1).astype(BF16)
            q = _rope_proj(x, _row(g_attn_pre[layer]), wqq, None, cos2, sin2, diff_scale)
            ot = _diff_attn(diff_lambda[jdx].astype(F32), q, k_sh, vt_sh,
                            diff_g_subln[jdx].reshape(-1, 1).astype(F32), _diff_lambda_init(layer))
            wo = diff_w_o[jdx].astype(BF16)
        x, h = _post_attn(ot.reshape(b, D_MODEL, s), wo, x, _row(g_attn_post[layer]), _row(g_ffn_pre[layer]))
        wup, cw, cb, wdn = _prep_ffn(ffn_w_up[layer], ffn_conv_w[layer], ffn_conv_b[layer], ffn_w_down[layer])
        x = _ffn(h, x, wup, cw, cb, wdn, _row(g_ffn_post[layer]))
    return x
```
